```python
import jax, jax.numpy as jnp
from jax import lax
import numpy as np

D_MODEL = 1024
BATCH = 16
SEQ = 2048
DEPTH = 4

HEAD_DIM = 64
ATTN_W = D_MODEL // 2
N_Q_HEADS = ATTN_W // HEAD_DIM
N_KV_HEADS = N_Q_HEADS // 4
GQA_GROUP = N_Q_HEADS // N_KV_HEADS
KV_W = N_KV_HEADS * HEAD_DIM
CONV_W = D_MODEL // 4
CONV_HEADS = CONV_W // HEAD_DIM
CONV_K = 3
FFT_W = D_MODEL // 4
FFT_GROUP_DIM = 64
FFT_GROUPS = FFT_W // FFT_GROUP_DIM
MIX_W = ATTN_W + CONV_W + FFT_W
IN_SIZES = (ATTN_W, KV_W, KV_W, ATTN_W, CONV_W, CONV_W, CONV_W, CONV_W, FFT_W, FFT_W)
IN_W = sum(IN_SIZES)
GRID_W = 64
ROPE_THETA = 10000.0
ROPE_HALF = HEAD_DIM // 2
Q_BLOCK = 128
RMS_EPS = 1e-6
LN_EPS = 1e-5
DEEPNORM_ALPHA = (2 * DEPTH) ** 0.25
DEEPNORM_BETA = (8 * DEPTH) ** -0.25
ADA_SCALE = 0.1

kernel_name = "hybrid_parallel_heads_deepnorm_encoder"


def _layer_norm(x, eps):
    xf = x.astype(jnp.float32)
    mu = jnp.mean(xf, axis=-1, keepdims=True)
    xc = xf - mu
    var = jnp.mean(xc * xc, axis=-1, keepdims=True)
    return (xc * lax.rsqrt(var + eps)).astype(x.dtype)


def _rms_norm(x, gain):
    xf = x.astype(jnp.float32)
    y = xf * lax.rsqrt(jnp.mean(xf * xf, axis=-1, keepdims=True) + RMS_EPS)
    return (y * gain.astype(jnp.float32)).astype(x.dtype)


def _axial_angles(seq_len):
    rows = seq_len // GRID_W
    row_pos = jnp.repeat(jnp.arange(rows, dtype=jnp.float32), GRID_W)
    col_pos = jnp.tile(jnp.arange(GRID_W, dtype=jnp.float32), rows)
    inv_freq = 1.0 / (ROPE_THETA ** (jnp.arange(0, ROPE_HALF, 2, dtype=jnp.float32) / ROPE_HALF))
    return row_pos[:, None] * inv_freq, col_pos[:, None] * inv_freq


def _rope_half(x, ang):
    cos = jnp.cos(ang)[:, None, :].astype(x.dtype)
    sin = jnp.sin(ang)[:, None, :].astype(x.dtype)
    x1, x2 = jnp.split(x, 2, axis=-1)
    return jnp.concatenate([x1 * cos - x2 * sin, x2 * cos + x1 * sin], axis=-1)


def _axial_rope(x, row_ang, col_ang):
    xr, xc = jnp.split(x, 2, axis=-1)
    return jnp.concatenate([_rope_half(xr, row_ang), _rope_half(xc, col_ang)], axis=-1)


def _attention(q, k, v):
    b, s = q.shape[0], q.shape[1]
    nb = s // Q_BLOCK
    scale = HEAD_DIM ** -0.5
    qg = q.reshape(b, nb, Q_BLOCK, N_KV_HEADS, GQA_GROUP, HEAD_DIM).transpose(1, 0, 2, 3, 4, 5)

    def block(qb):
        scores = jnp.einsum('bqkgd,bskd->bkgqs', qb, k,
                            preferred_element_type=jnp.float32) * scale
        p = jax.nn.softmax(scores, axis=-1).astype(v.dtype)
        return jnp.einsum('bkgqs,bskd->bqkgd', p, v)

    o = lax.map(block, qg)
    return o.transpose(1, 0, 2, 3, 4, 5).reshape(b, s, ATTN_W)


def _short_conv(u, w):
    s = u.shape[1]
    pad = CONV_K // 2
    up = jnp.pad(u, ((0, 0), (pad, pad), (0, 0)))
    return sum(up[:, j:j + s, :] * w[j] for j in range(CONV_K))


def _fourier(u, f_mix):
    b, s = u.shape[0], u.shape[1]
    ug = u.reshape(b, s, FFT_GROUPS, FFT_GROUP_DIM).astype(jnp.float32)
    f = jnp.fft.fft2(ug, axes=(1, 3), norm='ortho').real.astype(u.dtype)
    f = jnp.einsum('bsgc,gcd->bsgd', f, f_mix)
    return f.reshape(b, s, FFT_W)


def _split_indices():
    return np.cumsum(np.array(IN_SIZES))[:-1].tolist()


def setup_inputs(seed: int = 0) -> dict:
    key = jax.random.key(seed)
    ks = jax.random.split(key, 12)
    f32 = jnp.float32
    x = jax.random.normal(ks[0], (BATCH, SEQ, D_MODEL), f32)
    c = jax.random.normal(ks[1], (BATCH, D_MODEL), f32)
    w_ada = jax.random.normal(ks[2], (DEPTH, D_MODEL, 3 * D_MODEL), f32) * (D_MODEL ** -0.5) * ADA_SCALE
    b_ada = jax.random.normal(ks[3], (DEPTH, 3 * D_MODEL), f32) * 0.01
    w_in = jax.random.normal(ks[4], (DEPTH, D_MODEL, IN_W), f32) * (D_MODEL ** -0.5)
    q_gain = 1.0 + 0.05 * jax.random.normal(ks[5], (DEPTH, HEAD_DIM), f32)
    k_gain = 1.0 + 0.05 * jax.random.normal(ks[6], (DEPTH, HEAD_DIM), f32)
    conv_w = jax.random.normal(ks[7], (DEPTH, CONV_K, CONV_W), f32) * (CONV_K ** -0.5)
    f_mix = jax.random.normal(ks[8], (DEPTH, FFT_GROUPS, FFT_GROUP_DIM, FFT_GROUP_DIM), f32) * (FFT_GROUP_DIM ** -0.5)
    w_out = jax.random.normal(ks[9], (DEPTH, MIX_W, D_MODEL), f32) * (MIX_W ** -0.5) * DEEPNORM_BETA
    ln_g = 1.0 + 0.05 * jax.random.normal(ks[10], (DEPTH, D_MODEL), f32)
    ln_b = 0.01 * jax.random.normal(ks[11], (DEPTH, D_MODEL), f32)
    return {"x": x, "c": c, "w_ada": w_ada, "b_ada": b_ada, "w_in": w_in,
            "q_gain": q_gain, "k_gain": k_gain, "conv_w": conv_w, "f_mix": f_mix,
            "w_out": w_out, "ln_g": ln_g, "ln_b": ln_b}


def reference(x, c, w_ada, b_ada, w_in, q_gain, k_gain, conv_w, f_mix, w_out, ln_g, ln_b):
    b, s, _ = x.shape
    row_ang, col_ang = _axial_angles(s)
    split_idx = _split_indices()
    for l in range(DEPTH):
        mod = c @ w_ada[l] + b_ada[l]
        shift, scale, gate = jnp.split(mod, 3, axis=-1)
        h = _layer_norm(x, LN_EPS) * (1.0 + scale[:, None, :]) + shift[:, None, :]

        proj = h @ w_in[l]
        q, k, v, g_a, cu, cb, cc, g_c, fu, g_f = jnp.split(proj, split_idx, axis=-1)

        q = _axial_rope(_rms_norm(q.reshape(b, s, N_Q_HEADS, HEAD_DIM), q_gain[l]), row_ang, col_ang)
        k = _axial_rope(_rms_norm(k.reshape(b, s, N_KV_HEADS, HEAD_DIM), k_gain[l]), row_ang, col_ang)
        v = v.reshape(b, s, N_KV_HEADS, HEAD_DIM)
        y_attn = _attention(q, k, v) * jax.nn.silu(g_a)

        y_conv = cb * _short_conv(cc * cu, conv_w[l]) * jax.nn.silu(g_c)

        y_four = _fourier(fu, f_mix[l]) * jax.nn.silu(g_f)

        y = jnp.concatenate([y_attn, y_conv, y_four], axis=-1) @ w_out[l]

        x = _layer_norm(DEEPNORM_ALPHA * x + (1.0 + gate[:, None, :]) * y, LN_EPS) * ln_g[l] + ln_b[l]
    return x
```

```python
import functools

import jax
import jax.numpy as jnp
import numpy as np
from jax import lax
from jax.experimental import pallas as pl
from jax.experimental.pallas import tpu as pltpu

F32 = jnp.float32
BF16 = jnp.bfloat16

HEAD_DIM = 64
N_Q_HEADS = 8
N_KV_HEADS = 2
GQA_GROUP = N_Q_HEADS // N_KV_HEADS
ATTN_W = N_Q_HEADS * HEAD_DIM
KV_W = N_KV_HEADS * HEAD_DIM
CONV_W = 256
CONV_K = 3
FFT_W = 256
FFT_GROUP_DIM = 64
FFT_GROUPS = FFT_W // FFT_GROUP_DIM
GRID_W = 64
ROPE_THETA = 10000.0
ROPE_HALF = HEAD_DIM // 2
ROPE_QUARTER = ROPE_HALF // 2
RMS_EPS = 1e-6
LN_EPS = 1e-5

V7X_VMEM_LIMIT_BYTES = 56 * 1024 * 1024
F32_SUBLANES = 8

NAT_W = ATTN_W + 4 * CONV_W + 2 * FFT_W
TR_W = ATTN_W + 2 * KV_W


def _layer_norm_rows(x):
    mu = jnp.mean(x, axis=-1, keepdims=True)
    xc = x - mu
    var = jnp.mean(xc * xc, axis=-1, keepdims=True)
    return xc * lax.rsqrt(var + LN_EPS)


def _silu(x):
    return x * (1.0 / (1.0 + jnp.exp(-x)))


def _mod_kernel(c_ref, w_ref, b_ref, o_ref):
    o_ref[0] = jnp.dot(c_ref[...], w_ref[0], preferred_element_type=F32,
                       precision=lax.Precision.HIGHEST) + b_ref[0]


def _modulation(c, w_ada, b_ada):
    depth, d, d3 = w_ada.shape
    b = c.shape[0]
    return pl.pallas_call(
        _mod_kernel,
        grid=(depth, d3 // d),
        in_specs=[
            pl.BlockSpec((b, d), lambda l, j: (0, 0)),
            pl.BlockSpec((1, d, d), lambda l, j: (l, 0, j)),
            pl.BlockSpec((1, 1, d), lambda l, j: (l, 0, j)),
        ],
        out_specs=pl.BlockSpec((1, b, d), lambda l, j: (l, 0, j)),
        out_shape=jax.ShapeDtypeStruct((depth, b, d3), F32),
        name="adaln_mod",
    )(c, w_ada, b_ada.reshape(depth, 1, d3))


def _norm_rope_feature_major(xt, gain, rope):
    ms = jnp.mean(xt * xt, axis=0, keepdims=True)
    y = xt * lax.rsqrt(ms + RMS_EPS) * gain
    r = ROPE_QUARTER
    cos_r, sin_r = rope[0:r], rope[r:2 * r]
    cos_c, sin_c = rope[2 * r:3 * r], rope[3 * r:4 * r]
    x1r, x2r, x1c, x2c = y[0:r], y[r:2 * r], y[2 * r:3 * r], y[3 * r:4 * r]
    return jnp.concatenate([
        x1r * cos_r - x2r * sin_r,
        x2r * cos_r + x1r * sin_r,
        x1c * cos_c - x2c * sin_c,
        x2c * cos_c + x1c * sin_c,
    ], axis=0)


def _inproj_kernel(x_ref, sc_ref, sh_ref, wn_ref, wt_ref, gq_ref, gk_ref, rope_ref,
                   q_ref, k_ref, v_ref, sga_ref, u2_ref, cbg_ref, fu_ref, sgf_ref):
    h = _layer_norm_rows(x_ref[0]) * (1.0 + sc_ref[0]) + sh_ref[0]
    hb = h.astype(BF16)
    pn = jnp.dot(hb, wn_ref[...], preferred_element_type=F32)
    pt = lax.dot_general(wt_ref[...], hb, (((1,), (1,)), ((), ())),
                         preferred_element_type=F32)

    rope = rope_ref[...]
    gq = gq_ref[...] * (HEAD_DIM ** -0.5)
    gk = gk_ref[...]
    for hd in range(N_Q_HEADS):
        lo = hd * HEAD_DIM
        q_ref[0, lo:lo + HEAD_DIM, :] = _norm_rope_feature_major(
            pt[lo:lo + HEAD_DIM], gq, rope).astype(BF16)
    kt = jnp.concatenate([
        _norm_rope_feature_major(pt[ATTN_W + hd * HEAD_DIM:ATTN_W + (hd + 1) * HEAD_DIM], gk, rope)
        for hd in range(N_KV_HEADS)], axis=0)
    kn = kt.T
    for hd in range(N_KV_HEADS):
        k_ref[0, hd] = kn[:, hd * HEAD_DIM:(hd + 1) * HEAD_DIM].astype(BF16)
    v_ref[0] = pt[ATTN_W + KV_W:TR_W].astype(BF16)

    c0 = ATTN_W
    ga = pn[:, 0:c0]
    cu = pn[:, c0:c0 + CONV_W]
    cb = pn[:, c0 + CONV_W:c0 + 2 * CONV_W]
    cc = pn[:, c0 + 2 * CONV_W:c0 + 3 * CONV_W]
    gc = pn[:, c0 + 3 * CONV_W:c0 + 4 * CONV_W]
    f0 = c0 + 4 * CONV_W
    fu = pn[:, f0:f0 + FFT_W]
    gf = pn[:, f0 + FFT_W:f0 + 2 * FFT_W]
    sga_ref[0] = _silu(ga).astype(BF16)
    u2_ref[0] = cc * cu
    cbg_ref[0] = (cb * _silu(gc)).astype(BF16)
    fu_ref[0] = fu.astype(BF16)
    sgf_ref[0] = _silu(gf).astype(BF16)


def _in_projection(x, scale, shift, w_nat, w_tr, gq, gk, rope_t, ts):
    b, s, d = x.shape
    grid = (b, s // ts)
    tok = lambda w: pl.BlockSpec((1, ts, w), lambda i, j: (i, j, 0))
    const2 = lambda shape: pl.BlockSpec(shape, lambda i, j: (0, 0))
    out_shapes = (
        jax.ShapeDtypeStruct((b, ATTN_W, s), BF16),
        jax.ShapeDtypeStruct((b, N_KV_HEADS, s, HEAD_DIM), BF16),
        jax.ShapeDtypeStruct((b, KV_W, s), BF16),
        jax.ShapeDtypeStruct((b, s, ATTN_W), BF16),
        jax.ShapeDtypeStruct((b, s, CONV_W), F32),
        jax.ShapeDtypeStruct((b, s, CONV_W), BF16),
        jax.ShapeDtypeStruct((b, s, FFT_W), BF16),
        jax.ShapeDtypeStruct((b, s, FFT_W), BF16),
    )
    out_specs = (
        pl.BlockSpec((1, ATTN_W, ts), lambda i, j: (i, 0, j)),
        pl.BlockSpec((1, N_KV_HEADS, ts, HEAD_DIM), lambda i, j: (i, 0, j, 0)),
        pl.BlockSpec((1, KV_W, ts), lambda i, j: (i, 0, j)),
        tok(ATTN_W), tok(CONV_W), tok(CONV_W), tok(FFT_W), tok(FFT_W),
    )
    return pl.pallas_call(
        _inproj_kernel,
        grid=grid,
        in_specs=[
            tok(d),
            pl.BlockSpec((1, 1, d), lambda i, j: (i, 0, 0)),
            pl.BlockSpec((1, 1, d), lambda i, j: (i, 0, 0)),
            const2((d, NAT_W)),
            const2((TR_W, d)),
            const2((HEAD_DIM, 1)),
            const2((HEAD_DIM, 1)),
            pl.BlockSpec((HEAD_DIM, ts), lambda i, j: (0, j)),
        ],
        out_specs=out_specs,
        out_shape=out_shapes,
        compiler_params=pltpu.CompilerParams(
            dimension_semantics=("arbitrary", "arbitrary"),
            vmem_limit_bytes=V7X_VMEM_LIMIT_BYTES),
        name="in_projection",
    )(x, scale, shift, w_nat, w_tr, gq, gk, rope_t)


def _attn_kernel(q_ref, k_ref, v_ref, o_ref, *, tc):
    tt = q_ref.shape[-1]
    s_len = k_ref.shape[2]
    q = jnp.concatenate(
        [q_ref[0, g * HEAD_DIM:(g + 1) * HEAD_DIM, :] for g in range(GQA_GROUP)], axis=-1)
    n = GQA_GROUP * tt

    def body(c, carry):
        m, l, acc = carry
        start = pl.multiple_of(c * tc, tc)
        kc = k_ref[0, 0, pl.ds(start, tc), :]
        s = jnp.dot(kc, q, preferred_element_type=F32)
        m_new = jnp.maximum(m, jnp.max(s, axis=0, keepdims=True))
        p = jnp.exp(s - m_new)
        alpha = jnp.exp(m - m_new)
        l = alpha * l + jnp.sum(p, axis=0, keepdims=True)
        vc = v_ref[0, :, pl.ds(start, tc)]
        acc = alpha * acc + jnp.dot(vc, p.astype(BF16), preferred_element_type=F32)
        return m_new, l, acc

    init = (jnp.full((1, n), -jnp.inf, F32), jnp.zeros((1, n), F32),
            jnp.zeros((HEAD_DIM, n), F32))
    _, l, acc = lax.fori_loop(0, s_len // tc, body, init)
    o = acc / l
    ot = jnp.concatenate([o[:, g * tt:(g + 1) * tt] for g in range(GQA_GROUP)], axis=0)
    o_ref[0] = ot.T.astype(BF16)


def _attention(q_t, k, v_t, tt, tc):
    b, _, s = q_t.shape
    gw = GQA_GROUP * HEAD_DIM
    return pl.pallas_call(
        functools.partial(_attn_kernel, tc=tc),
        grid=(b, N_KV_HEADS, s // tt),
        in_specs=[
            pl.BlockSpec((1, gw, tt), lambda i, h, j: (i, h, j)),
            pl.BlockSpec((1, 1, s, HEAD_DIM), lambda i, h, j: (i, h, 0, 0)),
            pl.BlockSpec((1, HEAD_DIM, s), lambda i, h, j: (i, h, 0)),
        ],
        out_specs=pl.BlockSpec((1, tt, gw), lambda i, h, j: (i, j, h)),
        out_shape=jax.ShapeDtypeStruct((b, s, ATTN_W), BF16),
        compiler_params=pltpu.CompilerParams(
            dimension_semantics=("arbitrary", "arbitrary", "arbitrary"),
            vmem_limit_bytes=V7X_VMEM_LIMIT_BYTES),
        name="attention",
    )(q_t, k, v_t)


def _out_kernel(x_ref, o_ref, sga_ref, u2_ref, u2p_ref, u2n_ref, cbg_ref, fu_ref, sgf_ref,
                dc_ref, ds_ref, bdc_ref, bds_ref, fm_ref, wo_ref, gate_ref, cw_ref,
                lng_ref, lnb_ref, out_ref, *, alpha):
    t = pl.program_id(0)
    nt = pl.num_programs(0)
    ts = x_ref.shape[1]

    fu = fu_ref[0]
    a = jnp.dot(dc_ref[...], fu, preferred_element_type=F32)
    bm = jnp.dot(ds_ref[...], fu, preferred_element_type=F32)
    f = (jnp.dot(a.astype(BF16), bdc_ref[...], preferred_element_type=F32)
         - jnp.dot(bm.astype(BF16), bds_ref[...], preferred_element_type=F32))
    yf = jnp.dot(f.astype(BF16), fm_ref[...], preferred_element_type=F32) * sgf_ref[0].astype(F32)

    u2 = u2_ref[0]
    row = lax.broadcasted_iota(jnp.int32, u2.shape, 0)
    prev = jnp.where(t > 0, u2p_ref[0, F32_SUBLANES - 1:F32_SUBLANES, :], 0.0)
    nxt = jnp.where(t < nt - 1, u2n_ref[0, 0:1, :], 0.0)
    up = jnp.where(row == 0, prev, pltpu.roll(u2, 1, 0))
    dn = jnp.where(row == ts - 1, nxt, pltpu.roll(u2, ts - 1, 0))
    cw = cw_ref[...]
    conv = cw[0:1] * up + cw[1:2] * u2 + cw[2:3] * dn
    yc = cbg_ref[0].astype(F32) * conv

    ya = o_ref[0].astype(F32) * sga_ref[0].astype(F32)
    mix = jnp.concatenate([ya.astype(BF16), yc.astype(BF16), yf.astype(BF16)], axis=-1)
    y = jnp.dot(mix, wo_ref[...], preferred_element_type=F32)

    z = alpha * x_ref[0] + (1.0 + gate_ref[0]) * y
    out_ref[0] = _layer_norm_rows(z) * lng_ref[...] + lnb_ref[...]


def _output_stage(x, o, sga, u2, cbg, fu, sgf, dft_c, dft_s, bd_c, bd_s, fmix_bd, w_out,
                  gate, conv_w, ln_g, ln_b, ts, alpha):
    b, s, d = x.shape
    nt = s // ts
    halo = F32_SUBLANES
    tok = lambda w: pl.BlockSpec((1, ts, w), lambda t, i: (i, t, 0))
    const2 = lambda shape: pl.BlockSpec(shape, lambda t, i: (0, 0))
    return pl.pallas_call(
        functools.partial(_out_kernel, alpha=alpha),
        grid=(nt, b),
        in_specs=[
            tok(d), tok(ATTN_W), tok(ATTN_W), tok(CONV_W),
            pl.BlockSpec((1, halo, CONV_W),
                         lambda t, i: (i, jnp.maximum(t * (ts // halo) - 1, 0), 0)),
            pl.BlockSpec((1, halo, CONV_W),
                         lambda t, i: (i, jnp.minimum((t + 1) * (ts // halo), s // halo - 1), 0)),
            tok(CONV_W),
            pl.BlockSpec((1, s, FFT_W), lambda t, i: (i, 0, 0)),
            tok(FFT_W),
            pl.BlockSpec((ts, s), lambda t, i: (t, 0)),
            pl.BlockSpec((ts, s), lambda t, i: (t, 0)),
            const2((FFT_W, FFT_W)), const2((FFT_W, FFT_W)), const2((FFT_W, FFT_W)),
            const2((d, d)),
            pl.BlockSpec((1, 1, d), lambda t, i: (i, 0, 0)),
            const2((CONV_K, CONV_W)),
            const2((1, d)), const2((1, d)),
        ],
        out_specs=tok(d),
        out_shape=jax.ShapeDtypeStruct((b, s, d), F32),
        compiler_params=pltpu.CompilerParams(
            dimension_semantics=("arbitrary", "arbitrary"),
            vmem_limit_bytes=V7X_VMEM_LIMIT_BYTES),
        name="output_stage",
    )(x, o, sga, u2, u2, u2, cbg, fu, sgf, dft_c, dft_s, bd_c, bd_s, fmix_bd, w_out,
      gate, conv_w, ln_g, ln_b)


def _rope_table_feature_major(s):
    rows = s // GRID_W
    row_pos = jnp.repeat(jnp.arange(rows, dtype=F32), GRID_W)
    col_pos = jnp.tile(jnp.arange(GRID_W, dtype=F32), rows)
    inv_freq = 1.0 / (ROPE_THETA ** (jnp.arange(0, ROPE_HALF, 2, dtype=F32) / ROPE_HALF))
    row_ang = row_pos[None, :] * inv_freq[:, None]
    col_ang = col_pos[None, :] * inv_freq[:, None]
    return jnp.concatenate(
        [jnp.cos(row_ang), jnp.sin(row_ang), jnp.cos(col_ang), jnp.sin(col_ang)], axis=0)


def _dft_tables(n, scale):
    idx = jnp.arange(n, dtype=jnp.int32)
    r = (idx[:, None] * idx[None, :]) % n
    ang = r.astype(F32) * (2.0 * np.pi / n)
    return jnp.cos(ang) * scale, jnp.sin(ang) * scale


def _block_diag(blocks):
    g, n, _ = blocks.shape
    eye = jnp.eye(g, dtype=blocks.dtype)
    return (eye[:, None, :, None] * blocks[:, :, None, :]).reshape(g * n, g * n)


def kernel(x, c, w_ada, b_ada, w_in, q_gain, k_gain, conv_w, f_mix, w_out, ln_g, ln_b):
    b, s, d = x.shape
    depth = w_in.shape[0]
    alpha = (2 * depth) ** 0.25
    ts = min(512, s)
    tt = min(128, s)
    tc = min(256, s)

    mod = _modulation(c, w_ada, b_ada)
    rope_t = _rope_table_feature_major(s)
    dft_c, dft_s = _dft_tables(s, s ** -0.5)
    dft_c, dft_s = dft_c.astype(BF16), dft_s.astype(BF16)
    gc, gs = _dft_tables(FFT_GROUP_DIM, FFT_GROUP_DIM ** -0.5)
    bd_c = _block_diag(jnp.broadcast_to(gc, (FFT_GROUPS,) + gc.shape)).astype(BF16)
    bd_s = _block_diag(jnp.broadcast_to(gs, (FFT_GROUPS,) + gs.shape)).astype(BF16)

    qkv_w = ATTN_W + 2 * KV_W
    for l in range(depth):
        w_l = w_in[l]
        w_tr = w_l[:, :qkv_w].T.astype(BF16)
        w_nat = w_l[:, qkv_w:].astype(BF16)
        shift = mod[l, :, None, 0:d]
        scale = mod[l, :, None, d:2 * d]
        gate = mod[l, :, None, 2 * d:3 * d]
        q_t, k, v_t, sga, u2, cbg, fu, sgf = _in_projection(
            x, scale, shift, w_nat, w_tr, q_gain[l][:, None], k_gain[l][:, None], rope_t, ts)
        o = _attention(q_t, k, v_t, tt, tc)
        x = _output_stage(
            x, o, sga, u2, cbg, fu, sgf, dft_c, dft_s, bd_c, bd_s,
            _block_diag(f_mix[l]).astype(BF16), w_out[l].astype(BF16), gate,
            conv_w[l], ln_g[l][None, :], ln_b[l][None, :], ts, alpha)
    return x
```
